```python
import jax, jax.numpy as jnp
from jax import lax
import numpy as np

D_MODEL = 4096
BATCH = 4
SEQ = 4096
DEPTH = 4

CHUNK = 64
Q_BLOCK = 128
N_A_LAYERS = DEPTH // 2
N_B_LAYERS = DEPTH - N_A_LAYERS
SB_HEADS = 32
SB_HEAD_DIM = D_MODEL // SB_HEADS
MLA_HEADS = 64
Q_LORA_RANK = 1024
KV_LORA_RANK = 512
QK_NOPE_DIM = 128
QK_ROPE_DIM = 64
V_HEAD_DIM = 128
ROPE_THETA = 10000.0
N_EXPERTS = 64
TOP_K = 8
N_GROUPS = 8
TOPK_GROUPS = 4
D_EXPERT = 128
D_SHARED = TOP_K * D_EXPERT
ROUTED_SCALE = 2.5
PLE_DIM = 256
DN_ALPHA = (2.0 * DEPTH) ** 0.25
DN_BETA = (8.0 * DEPTH) ** -0.25
LN_EPS = 1e-5
RMS_EPS = 1e-6
NEG_INF = -1e30

kernel_name = "yoco_stickbreak_mla_moe_deepnorm_ple"


def layer_norm(x, g, b):
    xf = x.astype(jnp.float32)
    mu = jnp.mean(xf, axis=-1, keepdims=True)
    var = jnp.mean(jnp.square(xf - mu), axis=-1, keepdims=True)
    return ((xf - mu) * lax.rsqrt(var + LN_EPS) * g.astype(jnp.float32) + b.astype(jnp.float32)).astype(x.dtype)


def rms_norm(x, g):
    xf = x.astype(jnp.float32)
    ms = jnp.mean(jnp.square(xf), axis=-1, keepdims=True)
    return (xf * lax.rsqrt(ms + RMS_EPS) * g.astype(jnp.float32)).astype(x.dtype)


def rope_cos_sin(positions):
    half = QK_ROPE_DIM // 2
    inv_freq = ROPE_THETA ** (-jnp.arange(half, dtype=jnp.float32) / half)
    ang = positions.astype(jnp.float32)[..., None] * inv_freq
    return jnp.cos(ang), jnp.sin(ang)


def apply_rope(x, cos, sin):
    xf = x.astype(jnp.float32)
    x1, x2 = jnp.split(xf, 2, axis=-1)
    return jnp.concatenate([x1 * cos - x2 * sin, x2 * cos + x1 * sin], axis=-1).astype(x.dtype)


def stick_breaking_attention(x, w_qkv, w_o):
    B, S, D = x.shape
    qkv = (x @ w_qkv).reshape(B, S, 3, SB_HEADS, SB_HEAD_DIM)
    q, k, v = qkv[:, :, 0], qkv[:, :, 1], qkv[:, :, 2]
    scale = SB_HEAD_DIM ** -0.5
    outs = []
    for blk in range(S // Q_BLOCK):
        q0 = blk * Q_BLOCK
        kend = q0 + Q_BLOCK
        z = jnp.einsum('bqhd,bkhd->bhqk', q[:, q0:kend], k[:, :kend]).astype(jnp.float32) * scale
        t_idx = q0 + jnp.arange(Q_BLOCK)
        s_idx = jnp.arange(kend)
        mask = s_idx[None, :] < t_idx[:, None]
        log_one_minus_beta = jnp.where(mask, jax.nn.log_sigmoid(-z), 0.0)
        log_pass = lax.cumsum(log_one_minus_beta, axis=3, reverse=True) - log_one_minus_beta
        a = jnp.where(mask, jnp.exp(jax.nn.log_sigmoid(z) + log_pass), 0.0)
        outs.append(jnp.einsum('bhqk,bkhd->bqhd', a.astype(v.dtype), v[:, :kend]))
    o = jnp.concatenate(outs, axis=1).reshape(B, S, SB_HEADS * SB_HEAD_DIM)
    return o @ w_o


def shared_latent_kv(x, positions, w_dkv, kv_norm, w_ukv):
    B, S, _ = x.shape
    ckv = x @ w_dkv
    c, k_rope = ckv[..., :KV_LORA_RANK], ckv[..., KV_LORA_RANK:]
    c = rms_norm(c, kv_norm)
    cos, sin = rope_cos_sin(positions)
    k_rope = apply_rope(k_rope, cos, sin)
    kv = (c @ w_ukv).reshape(B, S, MLA_HEADS, QK_NOPE_DIM + V_HEAD_DIM)
    k_nope, v = kv[..., :QK_NOPE_DIM], kv[..., QK_NOPE_DIM:]
    return k_nope, v, k_rope


def mla_cross_attention(x, positions, w_dq, q_norm, w_uq, w_o, k_nope, v, k_rope):
    B, S, _ = x.shape
    cq = rms_norm(x @ w_dq, q_norm)
    q = (cq @ w_uq).reshape(B, S, MLA_HEADS, QK_NOPE_DIM + QK_ROPE_DIM)
    q_nope, q_rope = q[..., :QK_NOPE_DIM], q[..., QK_NOPE_DIM:]
    cos, sin = rope_cos_sin(positions)
    q_rope = apply_rope(q_rope, cos[:, :, None, :], sin[:, :, None, :])
    scale = (QK_NOPE_DIM + QK_ROPE_DIM) ** -0.5
    outs = []
    for blk in range(S // Q_BLOCK):
        q0 = blk * Q_BLOCK
        kend = q0 + Q_BLOCK
        s = (jnp.einsum('bqhd,bkhd->bhqk', q_nope[:, q0:kend], k_nope[:, :kend])
             + jnp.einsum('bqhr,bkr->bhqk', q_rope[:, q0:kend], k_rope[:, :kend])).astype(jnp.float32) * scale
        t_idx = q0 + jnp.arange(Q_BLOCK)
        s_idx = jnp.arange(kend)
        mask = (s_idx // CHUNK)[None, :] <= (t_idx // CHUNK)[:, None]
        prob = jax.nn.softmax(jnp.where(mask, s, NEG_INF), axis=-1)
        outs.append(jnp.einsum('bhqk,bkhd->bqhd', prob.astype(v.dtype), v[:, :kend]))
    o = jnp.concatenate(outs, axis=1).reshape(B, S, MLA_HEADS * V_HEAD_DIM)
    return o @ w_o


def moe_ffn(x, w_router, router_bias, w_exp_in, w_exp_down, w_shared_in, w_shared_down):
    B, S, D = x.shape
    n_tok = B * S
    t = x.reshape(n_tok, D)
    scores = jax.nn.sigmoid((t @ w_router).astype(jnp.float32))
    sel = scores + router_bias.astype(jnp.float32)
    per_group = N_EXPERTS // N_GROUPS
    group_score = jnp.sum(lax.top_k(sel.reshape(n_tok, N_GROUPS, per_group), 2)[0], axis=-1)
    _, g_idx = lax.top_k(group_score, TOPK_GROUPS)
    g_mask = jnp.any(g_idx[:, :, None] == jnp.arange(N_GROUPS)[None, None, :], axis=1)
    e_mask = jnp.repeat(g_mask, per_group, axis=1)
    _, e_idx = lax.top_k(jnp.where(e_mask, sel, NEG_INF), TOP_K)
    w = jnp.take_along_axis(scores, e_idx, axis=1)
    w = w / jnp.sum(w, axis=-1, keepdims=True) * ROUTED_SCALE
    gates = jnp.sum(jnp.where(e_idx[:, :, None] == jnp.arange(N_EXPERTS)[None, None, :], w[:, :, None], 0.0), axis=1)
    h = jnp.einsum('td,edf->tef', t, w_exp_in)
    hg, hu = jnp.split(h, 2, axis=-1)
    act = jax.nn.silu(hg) * hu * gates[:, :, None].astype(t.dtype)
    routed = jnp.einsum('tef,efd->td', act, w_exp_down)
    sg, su = jnp.split(t @ w_shared_in, 2, axis=-1)
    shared = (jax.nn.silu(sg) * su) @ w_shared_down
    return (routed + shared).reshape(B, S, D)


def per_layer_embedding(x, p_i, w_proj, w_gate):
    return (p_i @ w_proj) * jax.nn.sigmoid(x @ w_gate)


def setup_inputs(seed: int = 0) -> dict:
    key = jax.random.key(seed)
    ks = jax.random.split(key, 24)
    f32 = jnp.float32

    def nrm(k, shape, std):
        return jax.random.normal(k, shape, f32) * std

    D = D_MODEL
    x = nrm(ks[0], (BATCH, SEQ, D), 1.0)
    p = nrm(ks[1], (DEPTH, BATCH, SEQ, PLE_DIM), 1.0)
    offsets = jax.random.randint(ks[2], (BATCH, 1), 0, 4096, dtype=jnp.int32)
    positions = (offsets + jnp.arange(SEQ, dtype=jnp.int32)[None, :]).astype(jnp.int32)
    w_qkv_a = nrm(ks[3], (N_A_LAYERS, D, 3 * SB_HEADS * SB_HEAD_DIM), D ** -0.5)
    w_o_a = nrm(ks[4], (N_A_LAYERS, SB_HEADS * SB_HEAD_DIM, D), (SB_HEADS * SB_HEAD_DIM) ** -0.5 * DN_BETA)
    w_dq_b = nrm(ks[5], (N_B_LAYERS, D, Q_LORA_RANK), D ** -0.5)
    q_norm_b = 1.0 + nrm(ks[6], (N_B_LAYERS, Q_LORA_RANK), 0.02)
    w_uq_b = nrm(ks[7], (N_B_LAYERS, Q_LORA_RANK, MLA_HEADS * (QK_NOPE_DIM + QK_ROPE_DIM)), Q_LORA_RANK ** -0.5)
    w_o_b = nrm(ks[8], (N_B_LAYERS, MLA_HEADS * V_HEAD_DIM, D), (MLA_HEADS * V_HEAD_DIM) ** -0.5 * DN_BETA)
    w_dkv = nrm(ks[9], (D, KV_LORA_RANK + QK_ROPE_DIM), D ** -0.5)
    kv_norm = 1.0 + nrm(ks[10], (KV_LORA_RANK,), 0.02)
    w_ukv = nrm(ks[11], (KV_LORA_RANK, MLA_HEADS * (QK_NOPE_DIM + V_HEAD_DIM)), KV_LORA_RANK ** -0.5)
    ln_g = 1.0 + nrm(ks[12], (DEPTH, 3, D), 0.02)
    ln_b = nrm(ks[13], (DEPTH, 3, D), 0.02)
    w_router = nrm(ks[14], (DEPTH, D, N_EXPERTS), D ** -0.5)
    router_bias = nrm(ks[15], (DEPTH, N_EXPERTS), 0.01)
    w_exp_in = nrm(ks[16], (DEPTH, N_EXPERTS, D, 2 * D_EXPERT), D ** -0.5)
    w_exp_down = nrm(ks[17], (DEPTH, N_EXPERTS, D_EXPERT, D), D_EXPERT ** -0.5 * DN_BETA)
    w_shared_in = nrm(ks[18], (DEPTH, D, 2 * D_SHARED), D ** -0.5)
    w_shared_down = nrm(ks[19], (DEPTH, D_SHARED, D), D_SHARED ** -0.5 * DN_BETA)
    w_ple_proj = nrm(ks[20], (DEPTH, PLE_DIM, D), PLE_DIM ** -0.5 * DN_BETA)
    w_ple_gate = nrm(ks[21], (DEPTH, D, D), D ** -0.5)
    return {"x": x, "p": p, "positions": positions,
            "w_qkv_a": w_qkv_a, "w_o_a": w_o_a,
            "w_dq_b": w_dq_b, "q_norm_b": q_norm_b, "w_uq_b": w_uq_b, "w_o_b": w_o_b,
            "w_dkv": w_dkv, "kv_norm": kv_norm, "w_ukv": w_ukv,
            "ln_g": ln_g, "ln_b": ln_b,
            "w_router": w_router, "router_bias": router_bias,
            "w_exp_in": w_exp_in, "w_exp_down": w_exp_down,
            "w_shared_in": w_shared_in, "w_shared_down": w_shared_down,
            "w_ple_proj": w_ple_proj, "w_ple_gate": w_ple_gate}


def reference(x, p, positions, w_qkv_a, w_o_a, w_dq_b, q_norm_b, w_uq_b, w_o_b,
              w_dkv, kv_norm, w_ukv, ln_g, ln_b, w_router, router_bias,
              w_exp_in, w_exp_down, w_shared_in, w_shared_down, w_ple_proj, w_ple_gate):
    k_nope = v_shared = k_rope = None
    for i in range(DEPTH):
        if i < N_A_LAYERS:
            mix = stick_breaking_attention(x, w_qkv_a[i], w_o_a[i])
        else:
            j = i - N_A_LAYERS
            mix = mla_cross_attention(x, positions, w_dq_b[j], q_norm_b[j], w_uq_b[j], w_o_b[j],
                                      k_nope, v_shared, k_rope)
        x = layer_norm(DN_ALPHA * x + mix, ln_g[i, 0], ln_b[i, 0])
        ffn = moe_ffn(x, w_router[i], router_bias[i], w_exp_in[i], w_exp_down[i],
                      w_shared_in[i], w_shared_down[i])
        x = layer_norm(DN_ALPHA * x + ffn, ln_g[i, 1], ln_b[i, 1])
        ple = per_layer_embedding(x, p[i], w_ple_proj[i], w_ple_gate[i])
        x = layer_norm(DN_ALPHA * x + ple, ln_g[i, 2], ln_b[i, 2])
        if i == N_A_LAYERS - 1:
            k_nope, v_shared, k_rope = shared_latent_kv(x, positions, w_dkv, kv_norm, w_ukv)
    return x
```

```python
import functools

import jax
import jax.numpy as jnp
from jax import lax
from jax.experimental import pallas as pl
from jax.experimental.pallas import tpu as pltpu

F32 = jnp.float32
BF16 = jnp.bfloat16

SB_HEAD_DIM = 128
MLA_HEADS = 64
QK_NOPE_DIM = 128
QK_ROPE_DIM = 64
V_HEAD_DIM = 128
ROPE_THETA = 10000.0
CHUNK = 64
N_EXPERTS = 64
TOP_K = 8
N_GROUPS = 8
TOPK_GROUPS = 4
D_EXPERT = 128
ROUTED_SCALE = 2.5
LN_EPS = 1e-5
RMS_EPS = 1e-6
NEG_INF = -1e30

V7X_LANES = 128
V7X_VMEM_LIMIT_BYTES = 56 * 1024 * 1024

_NT = (((1,), (1,)), ((), ()))


def _pick(n, pref):
    return pref if n % pref == 0 else n


def _params(*sem):
    return pltpu.CompilerParams(dimension_semantics=sem, vmem_limit_bytes=V7X_VMEM_LIMIT_BYTES)


def _sigmoid(x):
    return 1.0 / (1.0 + jnp.exp(-x))


def _rope_pairs(r, c, sg):
    lane = lax.broadcasted_iota(jnp.int32, r.shape, 1)
    rot = jnp.where(lane % 64 < 32, pltpu.roll(r, 96, 1), pltpu.roll(r, 32, 1))
    return r * c + rot * sg


def _mm_kernel(*refs, nk, epilogue, n_extra):
    x_ref, w_ref = refs[0], refs[1]
    extra = refs[2:2 + n_extra]
    o_ref = refs[2 + n_extra]

    def finish(acc):
        if epilogue == "rms":
            ms = jnp.mean(acc * acc, axis=-1, keepdims=True)
            acc = acc * lax.rsqrt(ms + RMS_EPS) * extra[0][...]
        elif epilogue == "rope":
            c, sg = extra[0][...], extra[1][...]
            parts = [_rope_pairs(acc[:, g * V7X_LANES:(g + 1) * V7X_LANES], c, sg)
                     for g in range(acc.shape[1] // V7X_LANES)]
            acc = jnp.concatenate(parts, axis=1)
        o_ref[...] = acc.astype(o_ref.dtype)

    if nk == 1:
        finish(jnp.dot(x_ref[...], w_ref[...], preferred_element_type=F32))
    else:
        acc_ref = refs[3 + n_extra]
        k = pl.program_id(2)

        @pl.when(k == 0)
        def _():
            acc_ref[...] = jnp.zeros_like(acc_ref)

        acc_ref[...] += jnp.dot(x_ref[...], w_ref[...], preferred_element_type=F32)

        @pl.when(k == nk - 1)
        def _():
            finish(acc_ref[...])


def _mm(x, w, *, out_dtype, bm=1024, bn=1024, bk=4096, epilogue=None, extras=()):
    m, kdim = x.shape
    n = w.shape[1]
    bm, bn, bk = _pick(m, bm), _pick(n, bn), _pick(kdim, bk)
    nk = kdim // bk
    in_specs = [pl.BlockSpec((bm, bk), lambda i, j, k: (i, k)),
                pl.BlockSpec((bk, bn), lambda i, j, k: (k, j))]
    args = [x, w]
    for arr, bshape, imap in extras:
        in_specs.append(pl.BlockSpec(bshape, imap))
        args.append(arr)
    scratch = [pltpu.VMEM((bm, bn), F32)] if nk > 1 else []
    return pl.pallas_call(
        functools.partial(_mm_kernel, nk=nk, epilogue=epilogue, n_extra=len(extras)),
        grid=(m // bm, n // bn, nk),
        in_specs=in_specs,
        out_specs=pl.BlockSpec((bm, bn), lambda i, j, k: (i, j)),
        out_shape=jax.ShapeDtypeStruct((m, n), out_dtype),
        scratch_shapes=scratch,
        compiler_params=_params("parallel", "parallel", "arbitrary"),
    )(*args)


def _ln_kernel(x_ref, y_ref, g_ref, b_ref, o32_ref, o16_ref, *, alpha):
    h = alpha * x_ref[...] + y_ref[...].astype(F32)
    mu = jnp.mean(h, axis=-1, keepdims=True)
    d = h - mu
    var = jnp.mean(d * d, axis=-1, keepdims=True)
    o = d * lax.rsqrt(var + LN_EPS) * g_ref[...] + b_ref[...]
    o32_ref[...] = o
    o16_ref[...] = o.astype(BF16)


def _ln_residual(x, y, g, b, alpha):
    t, d = x.shape
    bm = _pick(t, 256)
    row = pl.BlockSpec((bm, d), lambda i: (i, 0))
    vec = pl.BlockSpec((1, d), lambda i: (0, 0))
    return pl.pallas_call(
        functools.partial(_ln_kernel, alpha=alpha),
        grid=(t // bm,),
        in_specs=[row, row, vec, vec],
        out_specs=[row, row],
        out_shape=[jax.ShapeDtypeStruct((t, d), F32), jax.ShapeDtypeStruct((t, d), BF16)],
        compiler_params=_params("parallel"),
    )(x, y, g.reshape(1, d), b.reshape(1, d))


def _sb_kernel(q_ref, k_ref, v_ref, u_ref, o_ref, *, blk, scale):
    i = pl.program_id(2)
    q = q_ref[...]
    u = u_ref[...]
    row = lax.broadcasted_iota(jnp.int32, (blk, blk), 0)
    col = lax.broadcasted_iota(jnp.int32, (blk, blk), 1)
    causal = col < row

    def block(kb, carry, masked):
        acc, run = carry
        start = pl.multiple_of(kb * blk, blk)
        k = k_ref[pl.ds(start, blk), :]
        v = v_ref[pl.ds(start, blk), :]
        z = lax.dot_general(q, k, _NT, preferred_element_type=F32) * scale
        sp = jnp.maximum(z, 0.0) + jnp.log(1.0 + jnp.exp(-jnp.abs(z)))
        if masked:
            sp = jnp.where(causal, sp, 0.0)
        hi = sp.astype(BF16)
        lo = (sp - hi.astype(F32)).astype(BF16)
        cum = (jnp.dot(hi, u, preferred_element_type=F32)
               + jnp.dot(lo, u, preferred_element_type=F32))
        a = jnp.exp(z - cum - run)
        if masked:
            a = jnp.where(causal, a, 0.0)
        acc = acc + jnp.dot(a.astype(BF16), v, preferred_element_type=F32)
        return acc, run + cum[:, 0:1]

    init = (jnp.zeros((blk, SB_HEAD_DIM), F32), jnp.zeros((blk, 1), F32))
    carry = block(i, init, True)
    carry = lax.fori_loop(0, i, lambda t, c: block(i - 1 - t, c, False), carry)
    o_ref[...] = carry[0].astype(o_ref.dtype)


def _sb_attention(qkv, batch, seq, heads):
    t = qkv.shape[0]
    blk = _pick(seq, 256)
    nq = seq // blk
    u = (jnp.arange(blk)[:, None] >= jnp.arange(blk)[None, :]).astype(BF16)
    d = SB_HEAD_DIM
    return pl.pallas_call(
        functools.partial(_sb_kernel, blk=blk, scale=d ** -0.5),
        grid=(batch, heads, nq),
        in_specs=[pl.BlockSpec((blk, d), lambda b, h, i: (b * nq + i, h)),
                  pl.BlockSpec((seq, d), lambda b, h, i: (b, heads + h)),
                  pl.BlockSpec((seq, d), lambda b, h, i: (b, 2 * heads + h)),
                  pl.BlockSpec((blk, blk), lambda b, h, i: (0, 0))],
        out_specs=pl.BlockSpec((blk, d), lambda b, h, i: (b * nq + i, h)),
        out_shape=jax.ShapeDtypeStruct((t, heads * d), BF16),
        compiler_params=_params("parallel", "parallel", "arbitrary"),
    )(qkv, qkv, qkv, u)


def _latent_kernel(x_ref, w_ref, g_ref, c_ref, sg_ref, oc_ref, or_ref, *, rank):
    res = jnp.dot(x_ref[...], w_ref[...], preferred_element_type=F32)
    c = res[:, :rank]
    ms = jnp.mean(c * c, axis=-1, keepdims=True)
    oc_ref[...] = (c * lax.rsqrt(ms + RMS_EPS) * g_ref[...]).astype(oc_ref.dtype)
    or_ref[...] = _rope_pairs(res[:, rank:], c_ref[...], sg_ref[...]).astype(or_ref.dtype)


def _latent_kv(xb, w_ext, kv_norm, cos_t, sin_t):
    t, d = xb.shape
    rank = kv_norm.shape[0]
    n = w_ext.shape[1]
    bm = _pick(t, 1024)
    return pl.pallas_call(
        functools.partial(_latent_kernel, rank=rank),
        grid=(t // bm,),
        in_specs=[pl.BlockSpec((bm, d), lambda i: (i, 0)),
                  pl.BlockSpec((d, n), lambda i: (0, 0)),
                  pl.BlockSpec((1, rank), lambda i: (0, 0)),
                  pl.BlockSpec((bm, V7X_LANES), lambda i: (i, 0)),
                  pl.BlockSpec((bm, V7X_LANES), lambda i: (i, 0))],
        out_specs=[pl.BlockSpec((bm, rank), lambda i: (i, 0)),
                   pl.BlockSpec((bm, V7X_LANES), lambda i: (i, 0))],
        out_shape=[jax.ShapeDtypeStruct((t, rank), BF16),
                   jax.ShapeDtypeStruct((t, V7X_LANES), BF16)],
        compiler_params=_params("parallel"),
    )(xb, w_ext, kv_norm.reshape(1, rank), cos_t, sin_t)


def _mla_kernel(qn_ref, qr_ref, kn_ref, kr_ref, v_ref, o_ref, kcat_ref, *, blk, scale):
    h = pl.program_id(1)
    i = pl.program_id(2)

    @pl.when(i == 0)
    def _():
        kcat_ref[:, :QK_NOPE_DIM] = kn_ref[...]
        kcat_ref[:, QK_NOPE_DIM:] = kr_ref[...]

    lane = lax.broadcasted_iota(jnp.int32, (blk, V7X_LANES), 1)
    qr = qr_ref[...]
    own = (lane >= QK_ROPE_DIM) == (h % 2 == 1)
    q = jnp.concatenate([qn_ref[...], jnp.where(own, qr, jnp.zeros_like(qr))], axis=1)

    row = lax.broadcasted_iota(jnp.int32, (blk, blk), 0)
    col = lax.broadcasted_iota(jnp.int32, (blk, blk), 1)
    visible = (col // CHUNK) <= (row // CHUNK)

    def block(kb, carry, masked):
        m, l, acc = carry
        start = pl.multiple_of(kb * blk, blk)
        k = kcat_ref[pl.ds(start, blk), :]
        v = v_ref[pl.ds(start, blk), :]
        s = lax.dot_general(q, k, _NT, preferred_element_type=F32) * scale
        if masked:
            s = jnp.where(visible, s, NEG_INF)
        m_new = jnp.maximum(m, jnp.max(s, axis=1, keepdims=True))
        alpha = jnp.exp(m - m_new)
        p = jnp.exp(s - m_new)
        l = alpha * l + jnp.sum(p, axis=1, keepdims=True)
        acc = alpha * acc + jnp.dot(p.astype(BF16), v, preferred_element_type=F32)
        return m_new, l, acc

    init = (jnp.full((blk, 1), NEG_INF, F32), jnp.zeros((blk, 1), F32),
            jnp.zeros((blk, V_HEAD_DIM), F32))
    carry = block(i, init, True)
    m, l, acc = lax.fori_loop(0, i, lambda t, c: block(t, c, False), carry)
    o_ref[...] = (acc / l).astype(o_ref.dtype)


def _mla_attention(qn, qr, kv, kr, batch, seq):
    t = qn.shape[0]
    heads = MLA_HEADS
    blk = _pick(seq, 256)
    nq = seq // blk
    scale = (QK_NOPE_DIM + QK_ROPE_DIM) ** -0.5
    return pl.pallas_call(
        functools.partial(_mla_kernel, blk=blk, scale=scale),
        grid=(batch, heads, nq),
        in_specs=[pl.BlockSpec((blk, QK_NOPE_DIM), lambda b, h, i: (b * nq + i, h)),
                  pl.BlockSpec((blk, V7X_LANES), lambda b, h, i: (b * nq + i, h // 2)),
                  pl.BlockSpec((seq, QK_NOPE_DIM), lambda b, h, i: (b, 2 * h)),
                  pl.BlockSpec((seq, V7X_LANES), lambda b, h, i: (b, 0)),
                  pl.BlockSpec((seq, V_HEAD_DIM), lambda b, h, i: (b, 2 * h + 1))],
        out_specs=pl.BlockSpec((blk, V_HEAD_DIM), lambda b, h, i: (b * nq + i, h)),
        out_shape=jax.ShapeDtypeStruct((t, heads * V_HEAD_DIM), BF16),
        scratch_shapes=[pltpu.VMEM((seq, QK_NOPE_DIM + V7X_LANES), BF16)],
        compiler_params=_params("parallel", "parallel", "arbitrary"),
    )(qn, qr, kv, kr, kv)


def _first_max(cur, idx, n):
    m = jnp.max(cur, axis=0, keepdims=True)
    first = jnp.min(jnp.where(cur == m, idx, n), axis=0, keepdims=True)
    return idx == first


def _router_kernel(x_ref, whi_ref, wlo_ref, bias_ref, g_ref):
    x = x_ref[...]
    xhi = x.astype(BF16)
    xlo = (x - xhi.astype(F32)).astype(BF16)
    whi, wlo = whi_ref[...], wlo_ref[...]
    logits = (lax.dot_general(whi, xhi, _NT, preferred_element_type=F32)
              + lax.dot_general(whi, xlo, _NT, preferred_element_type=F32)
              + lax.dot_general(wlo, xhi, _NT, preferred_element_type=F32))
    scores = _sigmoid(logits)
    sel = scores + bias_ref[...]
    bm = sel.shape[1]
    per = N_EXPERTS // N_GROUPS
    pidx = lax.broadcasted_iota(jnp.int32, (per, bm), 0)
    neg = jnp.float32(-jnp.inf)

    gscore = []
    for g in range(N_GROUPS):
        s = sel[g * per:(g + 1) * per, :]
        m1 = jnp.max(s, axis=0, keepdims=True)
        rest = jnp.where(_first_max(s, pidx, per), neg, s)
        gscore.append(m1 + jnp.max(rest, axis=0, keepdims=True))
    masked = []
    for g in range(N_GROUPS):
        rank = jnp.zeros((1, bm), jnp.int32)
        for o in range(N_GROUPS):
            if o == g:
                continue
            beats = (gscore[o] >= gscore[g]) if o < g else (gscore[o] > gscore[g])
            rank = rank + beats.astype(jnp.int32)
        keep = rank < TOPK_GROUPS
        masked.append(jnp.where(keep, sel[g * per:(g + 1) * per, :], NEG_INF))
    cur = jnp.concatenate(masked, axis=0)

    eidx = lax.broadcasted_iota(jnp.int32, (N_EXPERTS, bm), 0)
    chosen = jnp.zeros((N_EXPERTS, bm), jnp.bool_)
    for _ in range(TOP_K):
        pick = _first_max(cur, eidx, N_EXPERTS)
        chosen = jnp.logical_or(chosen, pick)
        cur = jnp.where(pick, neg, cur)
    w = jnp.where(chosen, scores, 0.0)
    gates = w / jnp.sum(w, axis=0, keepdims=True) * ROUTED_SCALE
    pad = jnp.concatenate([gates, jnp.ones((8, bm), F32),
                           jnp.zeros((V7X_LANES - N_EXPERTS - 8, bm), F32)], axis=0)
    g_ref[...] = pad.T


def _router(x32, w_hi_t, w_lo_t, bias):
    t, d = x32.shape
    bm = _pick(t, 512)
    return pl.pallas_call(
        _router_kernel,
        grid=(t // bm,),
        in_specs=[pl.BlockSpec((bm, d), lambda i: (i, 0)),
                  pl.BlockSpec((N_EXPERTS, d), lambda i: (0, 0)),
                  pl.BlockSpec((N_EXPERTS, d), lambda i: (0, 0)),
                  pl.BlockSpec((N_EXPERTS, 1), lambda i: (0, 0))],
        out_specs=pl.BlockSpec((bm, V7X_LANES), lambda i: (i, 0)),
        out_shape=jax.ShapeDtypeStruct((t, V7X_LANES), F32),
        compiler_params=_params("parallel"),
    )(x32, w_hi_t, w_lo_t, bias.reshape(N_EXPERTS, 1))


def _moe_kernel(x_ref, g_ref, w1_ref, w2_ref, o_ref, *, epc, ncol):
    j = pl.program_id(1)
    x = x_ref[...]
    gates = g_ref[...]
    lane = lax.broadcasted_iota(jnp.int32, gates.shape, 1)
    acts = []
    for e in range(epc):
        hcat = jnp.dot(x, w1_ref[e], preferred_element_type=F32)
        hg, hu = hcat[:, :D_EXPERT], hcat[:, D_EXPERT:]
        gate = jnp.sum(jnp.where(lane == j * epc + e, gates, 0.0), axis=1, keepdims=True)
        acts.append((hg * _sigmoid(hg) * hu * gate).astype(BF16))
    act = jnp.concatenate(acts, axis=1)
    w2 = w2_ref[...].reshape(epc * D_EXPERT, w2_ref.shape[2])
    d = w2.shape[1]
    first = j == 0
    for c in range(d // ncol):
        cs = slice(c * ncol, (c + 1) * ncol)
        part = jnp.dot(act, w2[:, cs], preferred_element_type=F32)
        o_ref[:, cs] = jnp.where(first, part, o_ref[:, cs] + part)


def _moe(xb, gates, w1, w2):
    t, d = xb.shape
    ne = w1.shape[0]
    bm = _pick(t, 512)
    epc = 4
    return pl.pallas_call(
        functools.partial(_moe_kernel, epc=epc, ncol=_pick(d, 512)),
        grid=(t // bm, ne // epc),
        in_specs=[pl.BlockSpec((bm, d), lambda i, j: (i, 0)),
                  pl.BlockSpec((bm, V7X_LANES), lambda i, j: (i, 0)),
                  pl.BlockSpec((epc, d, 2 * D_EXPERT), lambda i, j: (j, 0, 0)),
                  pl.BlockSpec((epc, D_EXPERT, d), lambda i, j: (j, 0, 0))],
        out_specs=pl.BlockSpec((bm, d), lambda i, j: (i, 0)),
        out_shape=jax.ShapeDtypeStruct((t, d), F32),
        compiler_params=_params("parallel", "arbitrary"),
    )(xb, gates, w1, w2)


def _ple_kernel(x_ref, wg_ref, p_ref, wp_ref, o_ref):
    gate = _sigmoid(jnp.dot(x_ref[...], wg_ref[...], preferred_element_type=F32))
    proj = jnp.dot(p_ref[...], wp_ref[...], preferred_element_type=F32)
    o_ref[...] = proj * gate


def _ple(xb, w_gate, pb, w_proj):
    t, d = xb.shape
    pd = pb.shape[1]
    bm, bn = _pick(t, 1024), _pick(d, 1024)
    return pl.pallas_call(
        _ple_kernel,
        grid=(t // bm, d // bn),
        in_specs=[pl.BlockSpec((bm, d), lambda i, j: (i, 0)),
                  pl.BlockSpec((d, bn), lambda i, j: (0, j)),
                  pl.BlockSpec((bm, pd), lambda i, j: (i, 0)),
                  pl.BlockSpec((pd, bn), lambda i, j: (0, j))],
        out_specs=pl.BlockSpec((bm, bn), lambda i, j: (i, j)),
        out_shape=jax.ShapeDtypeStruct((t, d), F32),
        compiler_params=_params("parallel", "parallel"),
    )(xb, w_gate, pb, w_proj)


def _rope_tables(positions):
    half = QK_ROPE_DIM // 2
    inv_freq = ROPE_THETA ** (-jnp.arange(half, dtype=F32) / half)
    ang = positions.reshape(-1).astype(F32)[:, None] * inv_freq
    cos, sin = jnp.cos(ang), jnp.sin(ang)
    return (jnp.concatenate([cos, cos, cos, cos], axis=1),
            jnp.concatenate([-sin, sin, -sin, sin], axis=1))


def kernel(x, p, positions, w_qkv_a, w_o_a, w_dq_b, q_norm_b, w_uq_b, w_o_b, w_dkv, kv_norm, w_ukv, ln_g, ln_b, w_router, router_bias, w_exp_in, w_exp_down, w_shared_in, w_shared_down, w_ple_proj, w_ple_gate):
    batch, seq, d = x.shape
    t = batch * seq
    depth = ln_g.shape[0]
    n_a = w_qkv_a.shape[0]
    alpha = (2.0 * depth) ** 0.25
    sb_heads = d // SB_HEAD_DIM
    rank_kv = kv_norm.shape[0]
    qk_dim = QK_NOPE_DIM + QK_ROPE_DIM
    d_shared = w_shared_down.shape[1]
    n_pseudo = d_shared // D_EXPERT

    cos_t, sin_t = _rope_tables(positions)
    x32 = x.reshape(t, d)
    xb = x32.astype(BF16)
    kv = kr = None

    for i in range(depth):
        if i < n_a:
            qkv = _mm(xb, w_qkv_a[i].astype(BF16), out_dtype=BF16)
            o = _sb_attention(qkv, batch, seq, sb_heads)
            mix = _mm(o, w_o_a[i].astype(BF16), out_dtype=F32)
        else:
            j = i - n_a
            rq = w_dq_b.shape[2]
            cq = _mm(xb, w_dq_b[j].astype(BF16), out_dtype=BF16, bn=rq, epilogue="rms",
                     extras=((q_norm_b[j].reshape(1, rq), (1, rq), lambda a, b, c: (0, 0)),))
            w_uq = w_uq_b[j].astype(BF16).reshape(rq, MLA_HEADS, qk_dim)
            w_nope = w_uq[:, :, :QK_NOPE_DIM].reshape(rq, MLA_HEADS * QK_NOPE_DIM)
            w_rope = w_uq[:, :, QK_NOPE_DIM:].reshape(rq, MLA_HEADS * QK_ROPE_DIM)
            qn = _mm(cq, w_nope, out_dtype=BF16)
            bm_r = _pick(t, 1024)
            qr = _mm(cq, w_rope, out_dtype=BF16, bm=bm_r, epilogue="rope",
                     extras=((cos_t, (bm_r, V7X_LANES), lambda a, b, c: (a, 0)),
                             (sin_t, (bm_r, V7X_LANES), lambda a, b, c: (a, 0))))
            o = _mla_attention(qn, qr, kv, kr, batch, seq)
            mix = _mm(o, w_o_b[j].astype(BF16), out_dtype=F32)
        x32, xb = _ln_residual(x32, mix, ln_g[i, 0], ln_b[i, 0], alpha)

        wr_t = w_router[i].T
        wr_hi = wr_t.astype(BF16)
        wr_lo = (wr_t - wr_hi.astype(F32)).astype(BF16)
        gates = _router(x32, wr_hi, wr_lo, router_bias[i])
        half = d_shared
        ws = w_shared_in[i].astype(BF16)
        ws = jnp.concatenate([ws[:, :half].reshape(d, n_pseudo, D_EXPERT),
                              ws[:, half:].reshape(d, n_pseudo, D_EXPERT)], axis=2)
        w1 = jnp.concatenate([w_exp_in[i].astype(BF16), ws.transpose(1, 0, 2)], axis=0)
        w2 = jnp.concatenate([w_exp_down[i].astype(BF16),
                              w_shared_down[i].astype(BF16).reshape(n_pseudo, D_EXPERT, d)], axis=0)
        ffn = _moe(xb, gates, w1, w2)
        x32, xb = _ln_residual(x32, ffn, ln_g[i, 1], ln_b[i, 1], alpha)

        ple = _ple(xb, w_ple_gate[i].astype(BF16), p[i].reshape(t, -1).astype(BF16),
                   w_ple_proj[i].astype(BF16))
        x32, xb = _ln_residual(x32, ple, ln_g[i, 2], ln_b[i, 2], alpha)

        if i == n_a - 1:
            w_c, w_r = w_dkv[:, :rank_kv], w_dkv[:, rank_kv:]
            w_ext = jnp.concatenate([w_c, w_r, w_r], axis=1).astype(BF16)
            c_lat, kr = _latent_kv(xb, w_ext, kv_norm, cos_t, sin_t)
            kv = _mm(c_lat, w_ukv.astype(BF16), out_dtype=BF16)
    return x32.reshape(batch, seq, d)
```

```python
import functools

import jax
import jax.numpy as jnp
from jax import lax
from jax.experimental import pallas as pl
from jax.experimental.pallas import tpu as pltpu

F32 = jnp.float32
BF16 = jnp.bfloat16

SB_HEAD_DIM = 128
MLA_HEADS = 64
QK_NOPE_DIM = 128
QK_ROPE_DIM = 64
V_HEAD_DIM = 128
ROPE_THETA = 10000.0
CHUNK = 64
N_EXPERTS = 64
TOP_K = 8
N_GROUPS = 8
TOPK_GROUPS = 4
D_EXPERT = 128
ROUTED_SCALE = 2.5
LN_EPS = 1e-5
RMS_EPS = 1e-6
NEG_INF = -1e30

V7X_LANES = 128
V7X_VMEM_LIMIT_BYTES = 56 * 1024 * 1024

SB_HEADS_PER_STEP = 8
MLA_HEADS_PER_STEP = 8
ATTN_BLOCK = 256

_NT = (((1,), (1,)), ((), ()))


def _pick(n, pref):
    return pref if n % pref == 0 else n


def _params(*sem):
    return pltpu.CompilerParams(dimension_semantics=sem, vmem_limit_bytes=V7X_VMEM_LIMIT_BYTES)


def _sigmoid(x):
    return 1.0 / (1.0 + jnp.exp(-x))


def _rope_pairs(r, c, sg):
    lane = lax.broadcasted_iota(jnp.int32, r.shape, 1)
    rot = jnp.where(lane % 64 < 32, pltpu.roll(r, 96, 1), pltpu.roll(r, 32, 1))
    return r * c + rot * sg


def _mm_kernel(*refs, nk, epilogue, n_extra):
    x_ref, w_ref = refs[0], refs[1]
    extra = refs[2:2 + n_extra]
    o_ref = refs[2 + n_extra]

    def finish(acc):
        if epilogue == "rms":
            ms = jnp.mean(acc * acc, axis=-1, keepdims=True)
            acc = acc * lax.rsqrt(ms + RMS_EPS) * extra[0][...]
        elif epilogue == "rope":
            c, sg = extra[0][...], extra[1][...]
            parts = [_rope_pairs(acc[:, g * V7X_LANES:(g + 1) * V7X_LANES], c, sg)
                     for g in range(acc.shape[1] // V7X_LANES)]
            acc = jnp.concatenate(parts, axis=1)
        o_ref[...] = acc.astype(o_ref.dtype)

    if nk == 1:
        finish(jnp.dot(x_ref[...], w_ref[...], preferred_element_type=F32))
    else:
        acc_ref = refs[3 + n_extra]
        k = pl.program_id(2)

        @pl.when(k == 0)
        def _():
            acc_ref[...] = jnp.zeros_like(acc_ref)

        acc_ref[...] += jnp.dot(x_ref[...], w_ref[...], preferred_element_type=F32)

        @pl.when(k == nk - 1)
        def _():
            finish(acc_ref[...])


def _mm(x, w, *, out_dtype, bm=1024, bn=1024, bk=4096, epilogue=None, extras=()):
    m, kdim = x.shape
    n = w.shape[1]
    bm, bn, bk = _pick(m, bm), _pick(n, bn), _pick(kdim, bk)
    nk = kdim // bk
    in_specs = [pl.BlockSpec((bm, bk), lambda i, j, k: (i, k)),
                pl.BlockSpec((bk, bn), lambda i, j, k: (k, j))]
    args = [x, w]
    for arr, bshape, imap in extras:
        in_specs.append(pl.BlockSpec(bshape, imap))
        args.append(arr)
    scratch = [pltpu.VMEM((bm, bn), F32)] if nk > 1 else []
    return pl.pallas_call(
        functools.partial(_mm_kernel, nk=nk, epilogue=epilogue, n_extra=len(extras)),
        grid=(m // bm, n // bn, nk),
        in_specs=in_specs,
        out_specs=pl.BlockSpec((bm, bn), lambda i, j, k: (i, j)),
        out_shape=jax.ShapeDtypeStruct((m, n), out_dtype),
        scratch_shapes=scratch,
        compiler_params=_params("parallel", "parallel", "arbitrary"),
    )(*args)


def _mm_vt_kernel(wt_ref, x_ref, o_ref, *, blk):
    res = lax.dot_general(wt_ref[...], x_ref[...], _NT, preferred_element_type=F32)
    for c in range(o_ref.shape[0]):
        o_ref[c] = res[:, c * blk:(c + 1) * blk].astype(o_ref.dtype)


def _mm_vt(wt, x, blk):
    n, kdim = wt.shape
    t = x.shape[0]
    bn, bt = _pick(n, 1024), _pick(t, 1024)
    return pl.pallas_call(
        functools.partial(_mm_vt_kernel, blk=blk),
        grid=(t // bt, n // bn),
        in_specs=[pl.BlockSpec((bn, kdim), lambda i, j: (j, 0)),
                  pl.BlockSpec((bt, kdim), lambda i, j: (i, 0))],
        out_specs=pl.BlockSpec((bt // blk, bn, blk), lambda i, j: (i, j, 0)),
        out_shape=jax.ShapeDtypeStruct((t // blk, n, blk), BF16),
        compiler_params=_params("parallel", "parallel"),
    )(wt, x)


def _ln_kernel(x_ref, y_ref, g_ref, b_ref, o32_ref, o16_ref, *, alpha):
    h = alpha * x_ref[...] + y_ref[...].astype(F32)
    mu = jnp.mean(h, axis=-1, keepdims=True)
    d = h - mu
    var = jnp.mean(d * d, axis=-1, keepdims=True)
    o = d * lax.rsqrt(var + LN_EPS) * g_ref[...] + b_ref[...]
    o32_ref[...] = o
    o16_ref[...] = o.astype(BF16)


def _ln_residual(x, y, g, b, alpha):
    t, d = x.shape
    bm = _pick(t, 256)
    row = pl.BlockSpec((bm, d), lambda i: (i, 0))
    vec = pl.BlockSpec((1, d), lambda i: (0, 0))
    return pl.pallas_call(
        functools.partial(_ln_kernel, alpha=alpha),
        grid=(t // bm,),
        in_specs=[row, row, vec, vec],
        out_specs=[row, row],
        out_shape=[jax.ShapeDtypeStruct((t, d), F32), jax.ShapeDtypeStruct((t, d), BF16)],
        compiler_params=_params("parallel"),
    )(x, y, g.reshape(1, d), b.reshape(1, d))


def _sb_kernel(q_ref, k_ref, vt_ref, tri_ref, o_ref, acc_ref, *, blk, scale, hps):
    i = pl.program_id(2)
    d = SB_HEAD_DIM
    tri = tri_ref[...]
    key = lax.broadcasted_iota(jnp.int32, (blk, blk), 0)
    qry = lax.broadcasted_iota(jnp.int32, (blk, blk), 1)
    causal = key < qry
    qs = [q_ref[:, a * d:(a + 1) * d] for a in range(hps)]

    def block(kb, carry, first):
        start = pl.multiple_of(kb * blk, blk)
        heads = range(hps)
        zs = [lax.dot_general(k_ref[pl.ds(start, blk), a * d:(a + 1) * d], qs[a], _NT,
                              preferred_element_type=F32) * scale for a in heads]
        his, los = [], []
        for a in heads:
            z = zs[a]
            sp = jnp.maximum(z, 0.0) + jnp.log(1.0 + jnp.exp(-jnp.abs(z)))
            if first:
                sp = jnp.where(causal, sp, 0.0)
            hi = sp.astype(BF16)
            his.append(hi)
            los.append((sp - hi.astype(F32)).astype(BF16))
        cums = [jnp.dot(tri, his[a], preferred_element_type=F32)
                + jnp.dot(tri, los[a], preferred_element_type=F32) for a in heads]
        ws = []
        for a in heads:
            w = jnp.exp(zs[a] - cums[a] - carry[a])
            if first:
                w = jnp.where(causal, w, 0.0)
            ws.append(w.astype(BF16))
        for a in heads:
            pv = jnp.dot(vt_ref[kb, a * d:(a + 1) * d, :], ws[a], preferred_element_type=F32)
            acc_ref[a] = pv if first else acc_ref[a] + pv
        return tuple(carry[a] + cums[a][0:1, :] for a in heads)

    carry = block(i, tuple(jnp.zeros((1, blk), F32) for _ in range(hps)), True)
    lax.fori_loop(0, i, lambda t, c: block(i - 1 - t, c, False), carry)
    for a in range(hps):
        o_ref[:, a * d:(a + 1) * d] = acc_ref[a].T.astype(o_ref.dtype)


def _sb_attention(qk, vt, batch, seq, heads):
    t = qk.shape[0]
    blk = vt.shape[2]
    nq = seq // blk
    hps = _pick(heads, SB_HEADS_PER_STEP)
    ng = heads // hps
    tri = (jnp.arange(blk)[None, :] >= jnp.arange(blk)[:, None]).astype(BF16)
    w = hps * SB_HEAD_DIM
    return pl.pallas_call(
        functools.partial(_sb_kernel, blk=blk, scale=SB_HEAD_DIM ** -0.5, hps=hps),
        grid=(batch, ng, nq),
        in_specs=[pl.BlockSpec((blk, w), lambda b, h, i: (b * nq + i, h)),
                  pl.BlockSpec((seq, w), lambda b, h, i: (b, ng + h)),
                  pl.BlockSpec((nq, w, blk), lambda b, h, i: (b, h, 0)),
                  pl.BlockSpec((blk, blk), lambda b, h, i: (0, 0))],
        out_specs=pl.BlockSpec((blk, w), lambda b, h, i: (b * nq + i, h)),
        out_shape=jax.ShapeDtypeStruct((t, heads * SB_HEAD_DIM), BF16),
        scratch_shapes=[pltpu.VMEM((hps, SB_HEAD_DIM, blk), F32)],
        name="sb_attn",
        compiler_params=_params("parallel", "parallel", "arbitrary"),
    )(qk, qk, vt, tri)


def _latent_kernel(x_ref, w_ref, g_ref, c_ref, sg_ref, oc_ref, or_ref, *, rank):
    res = jnp.dot(x_ref[...], w_ref[...], preferred_element_type=F32)
    c = res[:, :rank]
    ms = jnp.mean(c * c, axis=-1, keepdims=True)
    oc_ref[...] = (c * lax.rsqrt(ms + RMS_EPS) * g_ref[...]).astype(oc_ref.dtype)
    or_ref[...] = _rope_pairs(res[:, rank:], c_ref[...], sg_ref[...]).astype(or_ref.dtype)


def _latent_kv(xb, w_ext, kv_norm, cos_t, sin_t):
    t, d = xb.shape
    rank = kv_norm.shape[0]
    n = w_ext.shape[1]
    bm = _pick(t, 1024)
    return pl.pallas_call(
        functools.partial(_latent_kernel, rank=rank),
        grid=(t // bm,),
        in_specs=[pl.BlockSpec((bm, d), lambda i: (i, 0)),
                  pl.BlockSpec((d, n), lambda i: (0, 0)),
                  pl.BlockSpec((1, rank), lambda i: (0, 0)),
                  pl.BlockSpec((bm, V7X_LANES), lambda i: (i, 0)),
                  pl.BlockSpec((bm, V7X_LANES), lambda i: (i, 0))],
        out_specs=[pl.BlockSpec((bm, rank), lambda i: (i, 0)),
                   pl.BlockSpec((bm, V7X_LANES), lambda i: (i, 0))],
        out_shape=[jax.ShapeDtypeStruct((t, rank), BF16),
                   jax.ShapeDtypeStruct((t, V7X_LANES), BF16)],
        compiler_params=_params("parallel"),
    )(xb, w_ext, kv_norm.reshape(1, rank), cos_t, sin_t)


def _mla_kernel(qn_ref, qr_ref, kn_ref, kr_ref, vt_ref, o_ref, acc_ref, *, blk, scale, hps):
    i = pl.program_id(2)
    dn, dv = QK_NOPE_DIM, V_HEAD_DIM

    lane = lax.broadcasted_iota(jnp.int32, (blk, V7X_LANES), 1)
    qs = []
    for a in range(hps):
        qr = qr_ref[:, (a // 2) * V7X_LANES:(a // 2 + 1) * V7X_LANES]
        own = (lane >= QK_ROPE_DIM) if a % 2 else (lane < QK_ROPE_DIM)
        qs.append(jnp.concatenate([qn_ref[:, a * dn:(a + 1) * dn],
                                   jnp.where(own, qr, jnp.zeros_like(qr))], axis=1))

    key = lax.broadcasted_iota(jnp.int32, (blk, blk), 0)
    qry = lax.broadcasted_iota(jnp.int32, (blk, blk), 1)
    visible = (key // CHUNK) <= (qry // CHUNK)

    def block(kb, carry, first):
        start = pl.multiple_of(kb * blk, blk)
        out = []
        kr = kr_ref[pl.ds(start, blk), :]
        ss = [lax.dot_general(
            jnp.concatenate([kn_ref[pl.ds(start, blk), a * dn:(a + 1) * dn], kr], axis=1),
            qs[a], _NT, preferred_element_type=F32) for a in range(hps)]
        ps, alphas = [], []
        for a in range(hps):
            m, l = carry[a]
            s = ss[a] * scale
            if first:
                s = jnp.where(visible, s, NEG_INF)
            m_new = jnp.maximum(m, jnp.max(s, axis=0, keepdims=True))
            alpha = jnp.exp(m - m_new)
            p = jnp.exp(s - m_new)
            l = alpha * l + jnp.sum(p, axis=0, keepdims=True)
            ps.append(p.astype(BF16))
            alphas.append(alpha)
            out.append((m_new, l))
        for a in range(hps):
            vt = vt_ref[kb, a * dv:(a + 1) * dv, :]
            pv = jnp.dot(vt, ps[a], preferred_element_type=F32)
            acc_ref[a] = pv if first else alphas[a] * acc_ref[a] + pv
        return tuple(out)

    init = tuple((jnp.full((1, blk), NEG_INF, F32), jnp.zeros((1, blk), F32)) for _ in range(hps))
    carry = block(i, init, True)
    carry = lax.fori_loop(0, i, lambda t, c: block(t, c, False), carry)
    for a in range(hps):
        o = acc_ref[a] * (1.0 / carry[a][1])
        o_ref[:, a * dv:(a + 1) * dv] = o.T.astype(o_ref.dtype)


def _mla_attention(qn, qr, kn, kr, vt, batch, seq):
    t = qn.shape[0]
    heads = MLA_HEADS
    blk = vt.shape[2]
    nq = seq // blk
    hps = _pick(heads, MLA_HEADS_PER_STEP)
    scale = (QK_NOPE_DIM + QK_ROPE_DIM) ** -0.5
    return pl.pallas_call(
        functools.partial(_mla_kernel, blk=blk, scale=scale, hps=hps),
        grid=(batch, heads // hps, nq),
        in_specs=[pl.BlockSpec((blk, hps * QK_NOPE_DIM), lambda b, h, i: (b * nq + i, h)),
                  pl.BlockSpec((blk, hps * QK_ROPE_DIM), lambda b, h, i: (b * nq + i, h)),
                  pl.BlockSpec((seq, hps * QK_NOPE_DIM), lambda b, h, i: (b, h)),
                  pl.BlockSpec((seq, V7X_LANES), lambda b, h, i: (b, 0)),
                  pl.BlockSpec((nq, hps * V_HEAD_DIM, blk), lambda b, h, i: (b, h, 0))],
        out_specs=pl.BlockSpec((blk, hps * V_HEAD_DIM), lambda b, h, i: (b * nq + i, h)),
        out_shape=jax.ShapeDtypeStruct((t, heads * V_HEAD_DIM), BF16),
        scratch_shapes=[pltpu.VMEM((hps, V_HEAD_DIM, blk), F32)],
        name="mla_attn",
        compiler_params=_params("parallel", "parallel", "arbitrary"),
    )(qn, qr, kn, kr, vt)


def _first_max(cur, idx, n):
    m = jnp.max(cur, axis=0, keepdims=True)
    first = jnp.min(jnp.where(cur == m, idx, n), axis=0, keepdims=True)
    return idx == first


def _router_kernel(x_ref, whi_ref, wlo_ref, bias_ref, g_ref):
    x = x_ref[...]
    xhi = x.astype(BF16)
    xlo = (x - xhi.astype(F32)).astype(BF16)
    whi, wlo = whi_ref[...], wlo_ref[...]
    logits = (lax.dot_general(whi, xhi, _NT, preferred_element_type=F32)
              + lax.dot_general(whi, xlo, _NT, preferred_element_type=F32)
              + lax.dot_general(wlo, xhi, _NT, preferred_element_type=F32))
    scores = _sigmoid(logits)
    sel = scores + bias_ref[...]
    bm = sel.shape[1]
    per = N_EXPERTS // N_GROUPS
    pidx = lax.broadcasted_iota(jnp.int32, (per, bm), 0)
    neg = jnp.float32(-jnp.inf)

    gscore = []
    for g in range(N_GROUPS):
        s = sel[g * per:(g + 1) * per, :]
        m1 = jnp.max(s, axis=0, keepdims=True)
        rest = jnp.where(_first_max(s, pidx, per), neg, s)
        gscore.append(m1 + jnp.max(rest, axis=0, keepdims=True))
    masked = []
    for g in range(N_GROUPS):
        rank = jnp.zeros((1, bm), jnp.int32)
        for o in range(N_GROUPS):
            if o == g:
                continue
            beats = (gscore[o] >= gscore[g]) if o < g else (gscore[o] > gscore[g])
            rank = rank + beats.astype(jnp.int32)
        keep = rank < TOPK_GROUPS
        masked.append(jnp.where(keep, sel[g * per:(g + 1) * per, :], NEG_INF))
    cur = jnp.concatenate(masked, axis=0)

    eidx = lax.broadcasted_iota(jnp.int32, (N_EXPERTS, bm), 0)
    chosen = jnp.zeros((N_EXPERTS, bm), jnp.bool_)
    for _ in range(TOP_K):
        pick = _first_max(cur, eidx, N_EXPERTS)
        chosen = jnp.logical_or(chosen, pick)
        cur = jnp.where(pick, neg, cur)
    w = jnp.where(chosen, scores, 0.0)
    gates = w / jnp.sum(w, axis=0, keepdims=True) * ROUTED_SCALE
    pad = jnp.concatenate([gates, jnp.ones((8, bm), F32),
                           jnp.zeros((V7X_LANES - N_EXPERTS - 8, bm), F32)], axis=0)
    g_ref[...] = pad.T


def _router(x32, w_hi_t, w_lo_t, bias):
    t, d = x32.shape
    bm = _pick(t, 512)
    return pl.pallas_call(
        _router_kernel,
        grid=(t // bm,),
        in_specs=[pl.BlockSpec((bm, d), lambda i: (i, 0)),
                  pl.BlockSpec((N_EXPERTS, d), lambda i: (0, 0)),
                  pl.BlockSpec((N_EXPERTS, d), lambda i: (0, 0)),
                  pl.BlockSpec((N_EXPERTS, 1), lambda i: (0, 0))],
        out_specs=pl.BlockSpec((bm, V7X_LANES), lambda i: (i, 0)),
        out_shape=jax.ShapeDtypeStruct((t, V7X_LANES), F32),
        compiler_params=_params("parallel"),
    )(x32, w_hi_t, w_lo_t, bias.reshape(N_EXPERTS, 1))


def _moe_kernel(x_ref, g_ref, w1_ref, w2_ref, o_ref, *, epc, ncol):
    j = pl.program_id(1)
    x = x_ref[...]
    gates = g_ref[...]
    lane = lax.broadcasted_iota(jnp.int32, gates.shape, 1)
    acts = []
    for e in range(epc):
        hcat = jnp.dot(x, w1_ref[e], preferred_element_type=F32)
        hg, hu = hcat[:, :D_EXPERT], hcat[:, D_EXPERT:]
        gate = jnp.sum(jnp.where(lane == j * epc + e, gates, 0.0), axis=1, keepdims=True)
        acts.append((hg * _sigmoid(hg) * hu * gate).astype(BF16))
    act = jnp.concatenate(acts, axis=1)
    w2 = w2_ref[...].reshape(epc * D_EXPERT, w2_ref.shape[2])
    d = w2.shape[1]
    first = j == 0
    for c in range(d // ncol):
        cs = slice(c * ncol, (c + 1) * ncol)
        part = jnp.dot(act, w2[:, cs], preferred_element_type=F32)
        o_ref[:, cs] = jnp.where(first, part, o_ref[:, cs] + part)


def _moe(xb, gates, w1, w2):
    t, d = xb.shape
    ne = w1.shape[0]
    bm = _pick(t, 512)
    epc = 4
    return pl.pallas_call(
        functools.partial(_moe_kernel, epc=epc, ncol=_pick(d, 512)),
        grid=(t // bm, ne // epc),
        in_specs=[pl.BlockSpec((bm, d), lambda i, j: (i, 0)),
                  pl.BlockSpec((bm, V7X_LANES), lambda i, j: (i, 0)),
                  pl.BlockSpec((epc, d, 2 * D_EXPERT), lambda i, j: (j, 0, 0)),
                  pl.BlockSpec((epc, D_EXPERT, d), lambda i, j: (j, 0, 0))],
        out_specs=pl.BlockSpec((bm, d), lambda i, j: (i, 0)),
        out_shape=jax.ShapeDtypeStruct((t, d), F32),
        compiler_params=_params("parallel", "arbitrary"),
    )(xb, gates, w1, w2)


def _ple_kernel(x_ref, wg_ref, p_ref, wp_ref, o_ref):
    gate = _sigmoid(jnp.dot(x_ref[...], wg_ref[...], preferred_element_type=F32))
    proj = jnp.dot(p_ref[...], wp_ref[...], preferred_element_type=F32)
    o_ref[...] = proj * gate


def _ple(xb, w_gate, pb, w_proj):
    t, d = xb.shape
    pd = pb.shape[1]
    bm, bn = _pick(t, 1024), _pick(d, 1024)
    return pl.pallas_call(
        _ple_kernel,
        grid=(t // bm, d // bn),
        in_specs=[pl.BlockSpec((bm, d), lambda i, j: (i, 0)),
                  pl.BlockSpec((d, bn), lambda i, j: (0, j)),
                  pl.BlockSpec((bm, pd), lambda i, j: (i, 0)),
                  pl.BlockSpec((pd, bn), lambda i, j: (0, j))],
        out_specs=pl.BlockSpec((bm, bn), lambda i, j: (i, j)),
        out_shape=jax.ShapeDtypeStruct((t, d), F32),
        compiler_params=_params("parallel", "parallel"),
    )(xb, w_gate, pb, w_proj)


def _rope_tables(positions):
    half = QK_ROPE_DIM // 2
    inv_freq = ROPE_THETA ** (-jnp.arange(half, dtype=F32) / half)
    ang = positions.reshape(-1).astype(F32)[:, None] * inv_freq
    cos, sin = jnp.cos(ang), jnp.sin(ang)
    return (jnp.concatenate([cos, cos, cos, cos], axis=1),
            jnp.concatenate([-sin, sin, -sin, sin], axis=1))


def kernel(x, p, positions, w_qkv_a, w_o_a, w_dq_b, q_norm_b, w_uq_b, w_o_b, w_dkv, kv_norm, w_ukv, ln_g, ln_b, w_router, router_bias, w_exp_in, w_exp_down, w_shared_in, w_shared_down, w_ple_proj, w_ple_gate):
    batch, seq, d = x.shape
    t = batch * seq
    depth = ln_g.shape[0]
    n_a = w_qkv_a.shape[0]
    alpha = (2.0 * depth) ** 0.25
    sb_heads = d // SB_HEAD_DIM
    rank_kv = kv_norm.shape[0]
    qk_dim = QK_NOPE_DIM + QK_ROPE_DIM
    d_shared = w_shared_down.shape[1]
    n_pseudo = d_shared // D_EXPERT

    cos_t, sin_t = _rope_tables(positions)
    x32 = x.reshape(t, d)
    xb = x32.astype(BF16)
    kn = kr = vt_lat = None
    blk = _pick(seq, ATTN_BLOCK)

    for i in range(depth):
        if i < n_a:
            w_qkv = w_qkv_a[i].astype(BF16)
            qk = _mm(xb, w_qkv[:, :2 * d], out_dtype=BF16)
            vt = _mm_vt(w_qkv[:, 2 * d:].T, xb, blk)
            o = _sb_attention(qk, vt, batch, seq, sb_heads)
            mix = _mm(o, w_o_a[i].astype(BF16), out_dtype=F32)
        else:
            j = i - n_a
            rq = w_dq_b.shape[2]
            cq = _mm(xb, w_dq_b[j].astype(BF16), out_dtype=BF16, bn=rq, epilogue="rms",
                     extras=((q_norm_b[j].reshape(1, rq), (1, rq), lambda a, b, c: (0, 0)),))
            w_uq = w_uq_b[j].astype(BF16).reshape(rq, MLA_HEADS, qk_dim)
            w_nope = w_uq[:, :, :QK_NOPE_DIM].reshape(rq, MLA_HEADS * QK_NOPE_DIM)
            w_rope = w_uq[:, :, QK_NOPE_DIM:].reshape(rq, MLA_HEADS * QK_ROPE_DIM)
            qn = _mm(cq, w_nope, out_dtype=BF16)
            bm_r = _pick(t, 1024)
            qr = _mm(cq, w_rope, out_dtype=BF16, bm=bm_r, epilogue="rope",
                     extras=((cos_t, (bm_r, V7X_LANES), lambda a, b, c: (a, 0)),
                             (sin_t, (bm_r, V7X_LANES), lambda a, b, c: (a, 0))))
            o = _mla_attention(qn, qr, kn, kr, vt_lat, batch, seq)
            mix = _mm(o, w_o_b[j].astype(BF16), out_dtype=F32)
        x32, xb = _ln_residual(x32, mix, ln_g[i, 0], ln_b[i, 0], alpha)

        wr_t = w_router[i].T
        wr_hi = wr_t.astype(BF16)
        wr_lo = (wr_t - wr_hi.astype(F32)).astype(BF16)
        gates = _router(x32, wr_hi, wr_lo, router_bias[i])
        half = d_shared
        ws = w_shared_in[i].astype(BF16)
        ws = jnp.concatenate([ws[:, :half].reshape(d, n_pseudo, D_EXPERT),
                              ws[:, half:].reshape(d, n_pseudo, D_EXPERT)], axis=2)
        w1 = jnp.concatenate([w_exp_in[i].astype(BF16), ws.transpose(1, 0, 2)], axis=0)
        w2 = jnp.concatenate([w_exp_down[i].astype(BF16),
                              w_shared_down[i].astype(BF16).reshape(n_pseudo, D_EXPERT, d)], axis=0)
        ffn = _moe(xb, gates, w1, w2)
        x32, xb = _ln_residual(x32, ffn, ln_g[i, 1], ln_b[i, 1], alpha)

        ple = _ple(xb, w_ple_gate[i].astype(BF16), p[i].reshape(t, -1).astype(BF16),
                   w_ple_proj[i].astype(BF16))
        x32, xb = _ln_residual(x32, ple, ln_g[i, 2], ln_b[i, 2], alpha)

        if i == n_a - 1:
            w_c, w_r = w_dkv[:, :rank_kv], w_dkv[:, rank_kv:]
            w_ext = jnp.concatenate([w_c, w_r, w_r], axis=1).astype(BF16)
            c_lat, kr = _latent_kv(xb, w_ext, kv_norm, cos_t, sin_t)
            w_u = w_ukv.astype(BF16).reshape(rank_kv, MLA_HEADS, QK_NOPE_DIM + V_HEAD_DIM)
            w_uk = w_u[:, :, :QK_NOPE_DIM].reshape(rank_kv, MLA_HEADS * QK_NOPE_DIM)
            w_uv = w_u[:, :, QK_NOPE_DIM:].reshape(rank_kv, MLA_HEADS * V_HEAD_DIM)
            kn = _mm(c_lat, w_uk, out_dtype=BF16)
            vt_lat = _mm_vt(w_uv.T, c_lat, blk)
    return x32.reshape(batch, seq, d)
```

```python
import functools

import jax
import jax.numpy as jnp
from jax import lax
from jax.experimental import pallas as pl
from jax.experimental.pallas import tpu as pltpu

F32 = jnp.float32
BF16 = jnp.bfloat16

SB_HEAD_DIM = 128
MLA_HEADS = 64
QK_NOPE_DIM = 128
QK_ROPE_DIM = 64
V_HEAD_DIM = 128
ROPE_THETA = 10000.0
CHUNK = 64
N_EXPERTS = 64
TOP_K = 8
N_GROUPS = 8
TOPK_GROUPS = 4
D_EXPERT = 128
ROUTED_SCALE = 2.5
LN_EPS = 1e-5
RMS_EPS = 1e-6
NEG_INF = -1e30
LOG2_E = 1.4426950408889634

V7X_LANES = 128
V7X_VMEM_LIMIT_BYTES = 56 * 1024 * 1024

SB_HEADS_PER_STEP = 8
MLA_HEADS_PER_STEP = 8
ATTN_BLOCK = 256

_NT = (((1,), (1,)), ((), ()))


def _pick(n, pref):
    return pref if n % pref == 0 else n


def _params(*sem):
    return pltpu.CompilerParams(dimension_semantics=sem, vmem_limit_bytes=V7X_VMEM_LIMIT_BYTES)


def _sigmoid(x):
    return 1.0 / (1.0 + jnp.exp(-x))


def _rope_pairs(r, c, sg):
    lane = lax.broadcasted_iota(jnp.int32, r.shape, 1)
    rot = jnp.where(lane % 64 < 32, pltpu.roll(r, 96, 1), pltpu.roll(r, 32, 1))
    return r * c + rot * sg


def _mm_kernel(*refs, nk, epilogue, n_extra):
    x_ref, w_ref = refs[0], refs[1]
    extra = refs[2:2 + n_extra]
    o_ref = refs[2 + n_extra]

    def finish(acc):
        if epilogue == "rms":
            ms = jnp.mean(acc * acc, axis=-1, keepdims=True)
            acc = acc * lax.rsqrt(ms + RMS_EPS) * extra[0][...]
        elif epilogue == "rope":
            c, sg = extra[0][...], extra[1][...]
            parts = [_rope_pairs(acc[:, g * V7X_LANES:(g + 1) * V7X_LANES], c, sg)
                     for g in range(acc.shape[1] // V7X_LANES)]
            acc = jnp.concatenate(parts, axis=1)
        o_ref[...] = acc.astype(o_ref.dtype)

    if nk == 1:
        finish(jnp.dot(x_ref[...], w_ref[...], preferred_element_type=F32))
    else:
        acc_ref = refs[3 + n_extra]
        k = pl.program_id(2)

        @pl.when(k == 0)
        def _():
            acc_ref[...] = jnp.zeros_like(acc_ref)

        acc_ref[...] += jnp.dot(x_ref[...], w_ref[...], preferred_element_type=F32)

        @pl.when(k == nk - 1)
        def _():
            finish(acc_ref[...])


def _mm(x, w, *, out_dtype, bm=1024, bn=1024, bk=4096, epilogue=None, extras=()):
    m, kdim = x.shape
    n = w.shape[1]
    bm, bn, bk = _pick(m, bm), _pick(n, bn), _pick(kdim, bk)
    nk = kdim // bk
    in_specs = [pl.BlockSpec((bm, bk), lambda i, j, k: (i, k)),
                pl.BlockSpec((bk, bn), lambda i, j, k: (k, j))]
    args = [x, w]
    for arr, bshape, imap in extras:
        in_specs.append(pl.BlockSpec(bshape, imap))
        args.append(arr)
    scratch = [pltpu.VMEM((bm, bn), F32)] if nk > 1 else []
    return pl.pallas_call(
        functools.partial(_mm_kernel, nk=nk, epilogue=epilogue, n_extra=len(extras)),
        grid=(m // bm, n // bn, nk),
        in_specs=in_specs,
        out_specs=pl.BlockSpec((bm, bn), lambda i, j, k: (i, j)),
        out_shape=jax.ShapeDtypeStruct((m, n), out_dtype),
        scratch_shapes=scratch,
        compiler_params=_params("parallel", "parallel", "arbitrary"),
    )(*args)


def _mm_vt_kernel(wt_ref, x_ref, o_ref, *, blk):
    res = lax.dot_general(wt_ref[...], x_ref[...], _NT, preferred_element_type=F32)
    for c in range(o_ref.shape[0]):
        o_ref[c] = res[:, c * blk:(c + 1) * blk].astype(o_ref.dtype)


def _mm_vt(wt, x, blk):
    n, kdim = wt.shape
    t = x.shape[0]
    bn, bt = _pick(n, 1024), _pick(t, 1024)
    return pl.pallas_call(
        functools.partial(_mm_vt_kernel, blk=blk),
        grid=(t // bt, n // bn),
        in_specs=[pl.BlockSpec((bn, kdim), lambda i, j: (j, 0)),
                  pl.BlockSpec((bt, kdim), lambda i, j: (i, 0))],
        out_specs=pl.BlockSpec((bt // blk, bn, blk), lambda i, j: (i, j, 0)),
        out_shape=jax.ShapeDtypeStruct((t // blk, n, blk), BF16),
        compiler_params=_params("parallel", "parallel"),
    )(wt, x)


def _ln_kernel(x_ref, y_ref, g_ref, b_ref, o32_ref, o16_ref, *, alpha):
    h = alpha * x_ref[...] + y_ref[...].astype(F32)
    mu = jnp.mean(h, axis=-1, keepdims=True)
    d = h - mu
    var = jnp.mean(d * d, axis=-1, keepdims=True)
    o = d * lax.rsqrt(var + LN_EPS) * g_ref[...] + b_ref[...]
    o32_ref[...] = o
    o16_ref[...] = o.astype(BF16)


def _ln_residual(x, y, g, b, alpha):
    t, d = x.shape
    bm = _pick(t, 256)
    row = pl.BlockSpec((bm, d), lambda i: (i, 0))
    vec = pl.BlockSpec((1, d), lambda i: (0, 0))
    return pl.pallas_call(
        functools.partial(_ln_kernel, alpha=alpha),
        grid=(t // bm,),
        in_specs=[row, row, vec, vec],
        out_specs=[row, row],
        out_shape=[jax.ShapeDtypeStruct((t, d), F32), jax.ShapeDtypeStruct((t, d), BF16)],
        compiler_params=_params("parallel"),
    )(x, y, g.reshape(1, d), b.reshape(1, d))


LN_SLAB_ROWS = 32


def _store_cols_then_ln(h, j, nj, bn, g_ref, b_ref, o32_ref, o16_ref):
    for c in range(nj):
        @pl.when(j == c)
        def _(c=c):
            o32_ref[:, c * bn:(c + 1) * bn] = h

    @pl.when(j == nj - 1)
    def _():
        g, b = g_ref[...], b_ref[...]

        def slab(r, carry):
            rows = pl.ds(pl.multiple_of(r * LN_SLAB_ROWS, LN_SLAB_ROWS), LN_SLAB_ROWS)
            v = o32_ref[rows, :]
            mu = jnp.mean(v, axis=-1, keepdims=True)
            dv = v - mu
            var = jnp.mean(dv * dv, axis=-1, keepdims=True)
            o = dv * lax.rsqrt(var + LN_EPS) * g + b
            o32_ref[rows, :] = o
            o16_ref[rows, :] = o.astype(BF16)
            return carry

        lax.fori_loop(0, o32_ref.shape[0] // LN_SLAB_ROWS, slab, 0)


def _mm_ln_kernel(x_ref, w_ref, r_ref, g_ref, b_ref, o32_ref, o16_ref, *, alpha, nj, bn):
    h = alpha * r_ref[...] + jnp.dot(x_ref[...], w_ref[...], preferred_element_type=F32)
    _store_cols_then_ln(h, pl.program_id(1), nj, bn, g_ref, b_ref, o32_ref, o16_ref)


def _mm_ln(xa, w, res, g, b, alpha):
    m, kdim = xa.shape
    n = w.shape[1]
    bm = _pick(m, 512)
    bn = _pick(n, 2 * 1024 * 1024 // kdim)
    nj = n // bn
    row = pl.BlockSpec((bm, n), lambda i, j: (i, 0))
    vec = pl.BlockSpec((1, n), lambda i, j: (0, 0))
    return pl.pallas_call(
        functools.partial(_mm_ln_kernel, alpha=alpha, nj=nj, bn=bn),
        grid=(m // bm, nj),
        in_specs=[pl.BlockSpec((bm, kdim), lambda i, j: (i, 0)),
                  pl.BlockSpec((kdim, bn), lambda i, j: (0, j)),
                  pl.BlockSpec((bm, bn), lambda i, j: (i, j)),
                  vec, vec],
        out_specs=[row, row],
        out_shape=[jax.ShapeDtypeStruct((m, n), F32), jax.ShapeDtypeStruct((m, n), BF16)],
        compiler_params=_params("parallel", "arbitrary"),
    )(xa, w, res, g.reshape(1, n), b.reshape(1, n))


def _sb_kernel(q_ref, k_ref, vt_ref, tri_ref, o_ref, acc_ref, *, blk, scale, hps):
    i = pl.program_id(2)
    d = SB_HEAD_DIM
    tri = tri_ref[...]
    key = lax.broadcasted_iota(jnp.int32, (blk, blk), 0)
    qry = lax.broadcasted_iota(jnp.int32, (blk, blk), 1)
    causal = key < qry
    qts = [q_ref[:, a * d:(a + 1) * d].astype(F32).T.astype(BF16) for a in range(hps)]

    def block(kb, carry, first):
        start = pl.multiple_of(kb * blk, blk)
        heads = range(hps)
        zs = [jnp.dot(k_ref[pl.ds(start, blk), a * d:(a + 1) * d], qts[a],
                      preferred_element_type=F32) * scale for a in heads]
        his, los = [], []
        for a in heads:
            z = zs[a]
            sp = jnp.maximum(z, 0.0) + jnp.log(1.0 + jnp.exp(-jnp.abs(z)))
            if first:
                sp = jnp.where(causal, sp, 0.0)
            hi = sp.astype(BF16)
            his.append(hi)
            los.append((sp - hi.astype(F32)).astype(BF16))
        cums = [jnp.dot(tri, his[a], preferred_element_type=F32)
                + jnp.dot(tri, los[a], preferred_element_type=F32) for a in heads]
        ws = []
        for a in heads:
            w = jnp.exp(zs[a] - cums[a] - carry[a])
            if first:
                w = jnp.where(causal, w, 0.0)
            ws.append(w.astype(BF16))
        for a in heads:
            pv = jnp.dot(vt_ref[kb, a * d:(a + 1) * d, :], ws[a], preferred_element_type=F32)
            acc_ref[a] = pv if first else acc_ref[a] + pv
        return tuple(carry[a] + cums[a][0:1, :] for a in heads)

    carry = block(i, tuple(jnp.zeros((1, blk), F32) for _ in range(hps)), True)
    lax.fori_loop(0, i, lambda t, c: block(i - 1 - t, c, False), carry)
    for a in range(hps):
        o_ref[:, a * d:(a + 1) * d] = acc_ref[a].T.astype(o_ref.dtype)


def _sb_attention(qk, vt, batch, seq, heads):
    t = qk.shape[0]
    blk = vt.shape[2]
    nq = seq // blk
    hps = _pick(heads, SB_HEADS_PER_STEP)
    ng = heads // hps
    tri = (jnp.arange(blk)[None, :] >= jnp.arange(blk)[:, None]).astype(BF16)
    w = hps * SB_HEAD_DIM
    return pl.pallas_call(
        functools.partial(_sb_kernel, blk=blk, scale=SB_HEAD_DIM ** -0.5, hps=hps),
        grid=(batch, ng, nq),
        in_specs=[pl.BlockSpec((blk, w), lambda b, h, i: (b * nq + i, h)),
                  pl.BlockSpec((seq, w), lambda b, h, i: (b, ng + h)),
                  pl.BlockSpec((nq, w, blk), lambda b, h, i: (b, h, 0)),
                  pl.BlockSpec((blk, blk), lambda b, h, i: (0, 0))],
        out_specs=pl.BlockSpec((blk, w), lambda b, h, i: (b * nq + i, h)),
        out_shape=jax.ShapeDtypeStruct((t, heads * SB_HEAD_DIM), BF16),
        scratch_shapes=[pltpu.VMEM((hps, SB_HEAD_DIM, blk), F32)],
        name="sb_attn",
        compiler_params=_params("parallel", "parallel", "arbitrary"),
    )(qk, qk, vt, tri)


def _latent_kernel(x_ref, w_ref, g_ref, c_ref, sg_ref, oc_ref, or_ref, *, rank):
    res = jnp.dot(x_ref[...], w_ref[...], preferred_element_type=F32)
    c = res[:, :rank]
    ms = jnp.mean(c * c, axis=-1, keepdims=True)
    oc_ref[...] = (c * lax.rsqrt(ms + RMS_EPS) * g_ref[...]).astype(oc_ref.dtype)
    or_ref[...] = _rope_pairs(res[:, rank:], c_ref[...], sg_ref[...]).astype(or_ref.dtype)


def _latent_kv(xb, w_ext, kv_norm, cos_t, sin_t):
    t, d = xb.shape
    rank = kv_norm.shape[0]
    n = w_ext.shape[1]
    bm = _pick(t, 1024)
    return pl.pallas_call(
        functools.partial(_latent_kernel, rank=rank),
        grid=(t // bm,),
        in_specs=[pl.BlockSpec((bm, d), lambda i: (i, 0)),
                  pl.BlockSpec((d, n), lambda i: (0, 0)),
                  pl.BlockSpec((1, rank), lambda i: (0, 0)),
                  pl.BlockSpec((bm, V7X_LANES), lambda i: (i, 0)),
                  pl.BlockSpec((bm, V7X_LANES), lambda i: (i, 0))],
        out_specs=[pl.BlockSpec((bm, rank), lambda i: (i, 0)),
                   pl.BlockSpec((bm, V7X_LANES), lambda i: (i, 0))],
        out_shape=[jax.ShapeDtypeStruct((t, rank), BF16),
                   jax.ShapeDtypeStruct((t, V7X_LANES), BF16)],
        compiler_params=_params("parallel"),
    )(xb, w_ext, kv_norm.reshape(1, rank), cos_t, sin_t)


def _mla_kernel(qn_ref, qr_ref, kn_ref, kr_ref, vt_ref, o_ref, acc_ref, *, blk, scale, hps):
    i = pl.program_id(2)
    dn, dv = QK_NOPE_DIM, V_HEAD_DIM

    lane = lax.broadcasted_iota(jnp.int32, (blk, V7X_LANES), 1)
    qts = []
    for a in range(hps):
        qr = qr_ref[:, (a // 2) * V7X_LANES:(a // 2 + 1) * V7X_LANES]
        own = (lane >= QK_ROPE_DIM) if a % 2 else (lane < QK_ROPE_DIM)
        q = jnp.concatenate([qn_ref[:, a * dn:(a + 1) * dn],
                             jnp.where(own, qr, jnp.zeros_like(qr))], axis=1)
        qts.append(q.astype(F32).T.astype(BF16))

    key = lax.broadcasted_iota(jnp.int32, (blk, blk), 0)
    qry = lax.broadcasted_iota(jnp.int32, (blk, blk), 1)
    visible = (key // CHUNK) <= (qry // CHUNK)

    def block(kb, carry, first):
        start = pl.multiple_of(kb * blk, blk)
        out = []
        kr = kr_ref[pl.ds(start, blk), :]
        ss = [jnp.dot(
            jnp.concatenate([kn_ref[pl.ds(start, blk), a * dn:(a + 1) * dn], kr], axis=1),
            qts[a], preferred_element_type=F32) for a in range(hps)]
        ps, alphas = [], []
        for a in range(hps):
            m, l = carry[a]
            s = ss[a] * (scale * LOG2_E)
            if first:
                s = jnp.where(visible, s, NEG_INF)
            m_new = jnp.maximum(m, jnp.max(s, axis=0, keepdims=True))
            alpha = jnp.exp2(m - m_new)
            p = jnp.exp2(s - m_new)
            l = alpha * l + jnp.sum(p, axis=0, keepdims=True)
            ps.append(p.astype(BF16))
            alphas.append(alpha)
            out.append((m_new, l))
        for a in range(hps):
            vt = vt_ref[kb, a * dv:(a + 1) * dv, :]
            pv = jnp.dot(vt, ps[a], preferred_element_type=F32)
            acc_ref[a] = pv if first else alphas[a] * acc_ref[a] + pv
        return tuple(out)

    init = tuple((jnp.full((1, blk), NEG_INF, F32), jnp.zeros((1, blk), F32)) for _ in range(hps))
    carry = block(i, init, True)
    carry = lax.fori_loop(0, i, lambda t, c: block(t, c, False), carry)
    for a in range(hps):
        o = acc_ref[a] * (1.0 / carry[a][1])
        o_ref[:, a * dv:(a + 1) * dv] = o.T.astype(o_ref.dtype)


def _mla_attention(qn, qr, kn, kr, vt, batch, seq):
    t = qn.shape[0]
    heads = MLA_HEADS
    blk = vt.shape[2]
    nq = seq // blk
    hps = _pick(heads, MLA_HEADS_PER_STEP)
    scale = (QK_NOPE_DIM + QK_ROPE_DIM) ** -0.5
    return pl.pallas_call(
        functools.partial(_mla_kernel, blk=blk, scale=scale, hps=hps),
        grid=(batch, heads // hps, nq),
        in_specs=[pl.BlockSpec((blk, hps * QK_NOPE_DIM), lambda b, h, i: (b * nq + i, h)),
                  pl.BlockSpec((blk, hps * QK_ROPE_DIM), lambda b, h, i: (b * nq + i, h)),
                  pl.BlockSpec((seq, hps * QK_NOPE_DIM), lambda b, h, i: (b, h)),
                  pl.BlockSpec((seq, V7X_LANES), lambda b, h, i: (b, 0)),
                  pl.BlockSpec((nq, hps * V_HEAD_DIM, blk), lambda b, h, i: (b, h, 0))],
        out_specs=pl.BlockSpec((blk, hps * V_HEAD_DIM), lambda b, h, i: (b * nq + i, h)),
        out_shape=jax.ShapeDtypeStruct((t, heads * V_HEAD_DIM), BF16),
        scratch_shapes=[pltpu.VMEM((hps, V_HEAD_DIM, blk), F32)],
        name="mla_attn",
        compiler_params=_params("parallel", "parallel", "arbitrary"),
    )(qn, qr, kn, kr, vt)


def _first_max(cur, idx, n):
    m = jnp.max(cur, axis=0, keepdims=True)
    first = jnp.min(jnp.where(cur == m, idx, n), axis=0, keepdims=True)
    return idx == first


def _router_kernel(x_ref, whi_ref, wlo_ref, bias_ref, g_ref):
    x = x_ref[...]
    xhi = x.astype(BF16)
    xlo = (x - xhi.astype(F32)).astype(BF16)
    whi, wlo = whi_ref[...], wlo_ref[...]
    logits = (lax.dot_general(whi, xhi, _NT, preferred_element_type=F32)
              + lax.dot_general(whi, xlo, _NT, preferred_element_type=F32)
              + lax.dot_general(wlo, xhi, _NT, preferred_element_type=F32))
    scores = _sigmoid(logits)
    sel = scores + bias_ref[...]
    bm = sel.shape[1]
    per = N_EXPERTS // N_GROUPS
    pidx = lax.broadcasted_iota(jnp.int32, (per, bm), 0)
    neg = jnp.float32(-jnp.inf)

    gscore = []
    for g in range(N_GROUPS):
        s = sel[g * per:(g + 1) * per, :]
        m1 = jnp.max(s, axis=0, keepdims=True)
        rest = jnp.where(_first_max(s, pidx, per), neg, s)
        gscore.append(m1 + jnp.max(rest, axis=0, keepdims=True))
    masked = []
    for g in range(N_GROUPS):
        rank = jnp.zeros((1, bm), jnp.int32)
        for o in range(N_GROUPS):
            if o == g:
                continue
            beats = (gscore[o] >= gscore[g]) if o < g else (gscore[o] > gscore[g])
            rank = rank + beats.astype(jnp.int32)
        keep = rank < TOPK_GROUPS
        masked.append(jnp.where(keep, sel[g * per:(g + 1) * per, :], NEG_INF))
    cur = jnp.concatenate(masked, axis=0)

    eidx = lax.broadcasted_iota(jnp.int32, (N_EXPERTS, bm), 0)
    chosen = jnp.zeros((N_EXPERTS, bm), jnp.bool_)
    for _ in range(TOP_K):
        pick = _first_max(cur, eidx, N_EXPERTS)
        chosen = jnp.logical_or(chosen, pick)
        cur = jnp.where(pick, neg, cur)
    w = jnp.where(chosen, scores, 0.0)
    gates = w / jnp.sum(w, axis=0, keepdims=True) * ROUTED_SCALE
    pad = jnp.concatenate([gates, jnp.ones((8, bm), F32),
                           jnp.zeros((V7X_LANES - N_EXPERTS - 8, bm), F32)], axis=0)
    g_ref[...] = pad.T


def _router(x32, w_hi_t, w_lo_t, bias):
    t, d = x32.shape
    bm = _pick(t, 512)
    return pl.pallas_call(
        _router_kernel,
        grid=(t // bm,),
        in_specs=[pl.BlockSpec((bm, d), lambda i: (i, 0)),
                  pl.BlockSpec((N_EXPERTS, d), lambda i: (0, 0)),
                  pl.BlockSpec((N_EXPERTS, d), lambda i: (0, 0)),
                  pl.BlockSpec((N_EXPERTS, 1), lambda i: (0, 0))],
        out_specs=pl.BlockSpec((bm, V7X_LANES), lambda i: (i, 0)),
        out_shape=jax.ShapeDtypeStruct((t, V7X_LANES), F32),
        compiler_params=_params("parallel"),
    )(x32, w_hi_t, w_lo_t, bias.reshape(N_EXPERTS, 1))


def _moe_kernel(x_ref, g_ref, w1_ref, w2_ref, o_ref, *, epc, ncol):
    j = pl.program_id(1)
    x = x_ref[...]
    gates = g_ref[...]
    lane = lax.broadcasted_iota(jnp.int32, gates.shape, 1)
    acts = []
    for e in range(epc):
        hcat = jnp.dot(x, w1_ref[e], preferred_element_type=F32)
        hg, hu = hcat[:, :D_EXPERT], hcat[:, D_EXPERT:]
        gate = jnp.sum(jnp.where(lane == j * epc + e, gates, 0.0), axis=1, keepdims=True)
        acts.append((hg * _sigmoid(hg) * hu * gate).astype(BF16))
    act = jnp.concatenate(acts, axis=1)
    w2 = w2_ref[...].reshape(epc * D_EXPERT, w2_ref.shape[2])
    d = w2.shape[1]
    first = j == 0
    for c in range(d // ncol):
        cs = slice(c * ncol, (c + 1) * ncol)
        part = jnp.dot(act, w2[:, cs], preferred_element_type=F32)
        o_ref[:, cs] = jnp.where(first, part, o_ref[:, cs] + part)


def _moe(xb, gates, w1, w2):
    t, d = xb.shape
    ne = w1.shape[0]
    bm = _pick(t, 512)
    epc = 4
    return pl.pallas_call(
        functools.partial(_moe_kernel, epc=epc, ncol=_pick(d, 512)),
        grid=(t // bm, ne // epc),
        in_specs=[pl.BlockSpec((bm, d), lambda i, j: (i, 0)),
                  pl.BlockSpec((bm, V7X_LANES), lambda i, j: (i, 0)),
                  pl.BlockSpec((epc, d, 2 * D_EXPERT), lambda i, j: (j, 0, 0)),
                  pl.BlockSpec((epc, D_EXPERT, d), lambda i, j: (j, 0, 0))],
        out_specs=pl.BlockSpec((bm, d), lambda i, j: (i, 0)),
        out_shape=jax.ShapeDtypeStruct((t, d), F32),
        compiler_params=_params("parallel", "arbitrary"),
    )(xb, gates, w1, w2)


def _ple_ln_kernel(x_ref, wg_ref, p_ref, wp_ref, r_ref, g_ref, b_ref, o32_ref, o16_ref, *, alpha, nj, bn):
    gate = _sigmoid(jnp.dot(x_ref[...], wg_ref[...], preferred_element_type=F32))
    proj = jnp.dot(p_ref[...], wp_ref[...], preferred_element_type=F32)
    h = alpha * r_ref[...] + proj * gate
    _store_cols_then_ln(h, pl.program_id(1), nj, bn, g_ref, b_ref, o32_ref, o16_ref)


def _ple_ln(xb, w_gate, pb, w_proj, res, g, b, alpha):
    t, d = xb.shape
    pd = pb.shape[1]
    bm, bn = _pick(t, 512), _pick(d, 512)
    nj = d // bn
    row = pl.BlockSpec((bm, d), lambda i, j: (i, 0))
    vec = pl.BlockSpec((1, d), lambda i, j: (0, 0))
    return pl.pallas_call(
        functools.partial(_ple_ln_kernel, alpha=alpha, nj=nj, bn=bn),
        grid=(t // bm, nj),
        in_specs=[row,
                  pl.BlockSpec((d, bn), lambda i, j: (0, j)),
                  pl.BlockSpec((bm, pd), lambda i, j: (i, 0)),
                  pl.BlockSpec((pd, bn), lambda i, j: (0, j)),
                  pl.BlockSpec((bm, bn), lambda i, j: (i, j)),
                  vec, vec],
        out_specs=[row, row],
        out_shape=[jax.ShapeDtypeStruct((t, d), F32), jax.ShapeDtypeStruct((t, d), BF16)],
        compiler_params=_params("parallel", "arbitrary"),
    )(xb, w_gate, pb, w_proj, res, g.reshape(1, d), b.reshape(1, d))


def _rope_tables(positions):
    half = QK_ROPE_DIM // 2
    inv_freq = ROPE_THETA ** (-jnp.arange(half, dtype=F32) / half)
    ang = positions.reshape(-1).astype(F32)[:, None] * inv_freq
    cos, sin = jnp.cos(ang), jnp.sin(ang)
    return (jnp.concatenate([cos, cos, cos, cos], axis=1),
            jnp.concatenate([-sin, sin, -sin, sin], axis=1))


def kernel(x, p, positions, w_qkv_a, w_o_a, w_dq_b, q_norm_b, w_uq_b, w_o_b, w_dkv, kv_norm, w_ukv, ln_g, ln_b, w_router, router_bias, w_exp_in, w_exp_down, w_shared_in, w_shared_down, w_ple_proj, w_ple_gate):
    batch, seq, d = x.shape
    t = batch * seq
    depth = ln_g.shape[0]
    n_a = w_qkv_a.shape[0]
    alpha = (2.0 * depth) ** 0.25
    sb_heads = d // SB_HEAD_DIM
    rank_kv = kv_norm.shape[0]
    qk_dim = QK_NOPE_DIM + QK_ROPE_DIM
    d_shared = w_shared_down.shape[1]
    n_pseudo = d_shared // D_EXPERT

    cos_t, sin_t = _rope_tables(positions)
    x32 = x.reshape(t, d)
    xb = x32.astype(BF16)
    kn = kr = vt_lat = None
    blk = _pick(seq, ATTN_BLOCK)

    for i in range(depth):
        if i < n_a:
            w_qkv = w_qkv_a[i].astype(BF16)
            qk = _mm(xb, w_qkv[:, :2 * d], out_dtype=BF16)
            vt = _mm_vt(w_qkv[:, 2 * d:].T, xb, blk)
            o = _sb_attention(qk, vt, batch, seq, sb_heads)
            w_o = w_o_a[i].astype(BF16)
        else:
            j = i - n_a
            rq = w_dq_b.shape[2]
            cq = _mm(xb, w_dq_b[j].astype(BF16), out_dtype=BF16, bn=rq, epilogue="rms",
                     extras=((q_norm_b[j].reshape(1, rq), (1, rq), lambda a, b, c: (0, 0)),))
            w_uq = w_uq_b[j].astype(BF16).reshape(rq, MLA_HEADS, qk_dim)
            w_nope = w_uq[:, :, :QK_NOPE_DIM].reshape(rq, MLA_HEADS * QK_NOPE_DIM)
            w_rope = w_uq[:, :, QK_NOPE_DIM:].reshape(rq, MLA_HEADS * QK_ROPE_DIM)
            qn = _mm(cq, w_nope, out_dtype=BF16)
            bm_r = _pick(t, 1024)
            qr = _mm(cq, w_rope, out_dtype=BF16, bm=bm_r, epilogue="rope",
                     extras=((cos_t, (bm_r, V7X_LANES), lambda a, b, c: (a, 0)),
                             (sin_t, (bm_r, V7X_LANES), lambda a, b, c: (a, 0))))
            o = _mla_attention(qn, qr, kn, kr, vt_lat, batch, seq)
            w_o = w_o_b[j].astype(BF16)
        x32, xb = _mm_ln(o, w_o, x32, ln_g[i, 0], ln_b[i, 0], alpha)

        wr_t = w_router[i].T
        wr_hi = wr_t.astype(BF16)
        wr_lo = (wr_t - wr_hi.astype(F32)).astype(BF16)
        gates = _router(x32, wr_hi, wr_lo, router_bias[i])
        half = d_shared
        ws = w_shared_in[i].astype(BF16)
        ws = jnp.concatenate([ws[:, :half].reshape(d, n_pseudo, D_EXPERT),
                              ws[:, half:].reshape(d, n_pseudo, D_EXPERT)], axis=2)
        w1 = jnp.concatenate([w_exp_in[i].astype(BF16), ws.transpose(1, 0, 2)], axis=0)
        w2 = jnp.concatenate([w_exp_down[i].astype(BF16),
                              w_shared_down[i].astype(BF16).reshape(n_pseudo, D_EXPERT, d)], axis=0)
        ffn = _moe(xb, gates, w1, w2)
        x32, xb = _ln_residual(x32, ffn, ln_g[i, 1], ln_b[i, 1], alpha)

        x32, xb = _ple_ln(xb, w_ple_gate[i].astype(BF16), p[i].reshape(t, -1).astype(BF16),
                          w_ple_proj[i].astype(BF16), x32, ln_g[i, 2], ln_b[i, 2], alpha)

        if i == n_a - 1:
            w_c, w_r = w_dkv[:, :rank_kv], w_dkv[:, rank_kv:]
            w_ext = jnp.concatenate([w_c, w_r, w_r], axis=1).astype(BF16)
            c_lat, kr = _latent_kv(xb, w_ext, kv_norm, cos_t, sin_t)
            w_u = w_ukv.astype(BF16).reshape(rank_kv, MLA_HEADS, QK_NOPE_DIM + V_HEAD_DIM)
            w_uk = w_u[:, :, :QK_NOPE_DIM].reshape(rank_kv, MLA_HEADS * QK_NOPE_DIM)
            w_uv = w_u[:, :, QK_NOPE_DIM:].reshape(rank_kv, MLA_HEADS * V_HEAD_DIM)
            kn = _mm(c_lat, w_uk, out_dtype=BF16)
            vt_lat = _mm_vt(w_uv.T, c_lat, blk)
    return x32.reshape(batch, seq, d)
```
